```python
import functools
import jax, jax.numpy as jnp
from jax import lax
import numpy as np

D_MODEL = 1024
BATCH = 2
SEQ = 8192
DEPTH = 1
DEC_BATCH = 128
DEC_SEQ = 4
PAST_LEN = 8192
PAGE_SIZE = 128

CHUNK = 128
A_GROUPS = 4
A_WIDTH = D_MODEL // 2
A_GROUP_DIM = A_WIDTH // A_GROUPS
HEAD_DIM = 64
B_HEADS = (D_MODEL // 2) // HEAD_DIM
B_KV_HEADS = 2
B_GROUP = B_HEADS // B_KV_HEADS
WINDOW = 128
ROPE_THETA = 10000.0
N_BRANCH = 2
N_GROUPS = 4
EXPERTS_PER_GROUP = 8
N_EXPERTS = N_GROUPS * EXPERTS_PER_GROUP
TOP_K = 2
EXPERT_HIDDEN = D_MODEL // 4
MOE_BLOCK = 128
EPS = 1e-6
Q_WIDTH = B_HEADS * HEAD_DIM
KV_WIDTH = B_KV_HEADS * HEAD_DIM
PROJ_SPLITS = [A_WIDTH, 2 * A_WIDTH, 2 * A_WIDTH + Q_WIDTH, 2 * A_WIDTH + Q_WIDTH + KV_WIDTH, 2 * A_WIDTH + Q_WIDTH + 2 * KV_WIDTH]
PROJ_COLS = 2 * A_WIDTH + Q_WIDTH + 2 * KV_WIDTH + N_BRANCH * D_MODEL

kernel_name = "gated_parallel_chunkmlp_swa_hmoe_step"


def rms_norm(x, g):
    x32 = x.astype(jnp.float32)
    y = x32 * lax.rsqrt(jnp.mean(x32 * x32, axis=-1, keepdims=True) + EPS)
    return (y * g.astype(jnp.float32)).astype(x.dtype)


def layer_norm(x, g, b):
    x32 = x.astype(jnp.float32)
    mu = jnp.mean(x32, axis=-1, keepdims=True)
    xc = x32 - mu
    y = xc * lax.rsqrt(jnp.mean(xc * xc, axis=-1, keepdims=True) + EPS)
    return (y * g.astype(jnp.float32) + b.astype(jnp.float32)).astype(x.dtype)


def adaln_params(c, w_ada, b_ada):
    m = jax.nn.silu(c) @ w_ada + b_ada
    return [p[:, None, :] for p in jnp.split(m, 6, axis=-1)]


def rope(x, positions):
    half = HEAD_DIM // 2
    inv = jnp.power(jnp.float32(ROPE_THETA), -2.0 * jnp.arange(half, dtype=jnp.float32) / HEAD_DIM)
    ang = positions.astype(jnp.float32)[:, None] * inv[None, :]
    cos = jnp.cos(ang)[:, None, :]
    sin = jnp.sin(ang)[:, None, :]
    x32 = x.astype(jnp.float32)
    x1, x2 = x32[..., :half], x32[..., half:]
    return jnp.concatenate([x1 * cos - x2 * sin, x2 * cos + x1 * sin], axis=-1).astype(x.dtype)


def sink_softmax(s, sink):
    m = jnp.maximum(jnp.max(s, axis=-1, keepdims=True), sink)
    e = jnp.exp(s - m)
    z = jnp.sum(e, axis=-1, keepdims=True) + jnp.exp(sink - m)
    return e / z


def chunk_spatial_gate(u, v_n, w_s, b_s):
    n, l, wd = v_n.shape
    nc = -(-l // CHUNK)
    lp = nc * CHUNK
    vp = jnp.pad(v_n, ((0, 0), (0, lp - l), (0, 0))).reshape(n, nc, CHUNK, A_GROUPS, A_GROUP_DIM)
    w = w_s * jnp.tril(jnp.ones((CHUNK, CHUNK), w_s.dtype))
    mixed = jnp.einsum('gij,ncjgd->ncigd', w, vp) + b_s.T[None, None, :, :, None]
    return u * mixed.reshape(n, lp, wd)[:, :l]


def swa_prompt(q, k, v, sinks):
    n, s_len, _, dh = q.shape
    nb = s_len // WINDOW
    qb = q.reshape(n, nb, WINDOW, B_KV_HEADS, B_GROUP, dh)
    kb = k.reshape(n, nb, WINDOW, B_KV_HEADS, dh)
    vb = v.reshape(n, nb, WINDOW, B_KV_HEADS, dh)
    prev = lambda t: jnp.concatenate([jnp.zeros_like(t[:, :1]), t[:, :-1]], axis=1)
    kk = jnp.concatenate([prev(kb), kb], axis=2)
    vv = jnp.concatenate([prev(vb), vb], axis=2)
    s = jnp.einsum('bnqhgd,bnkhd->bnhgqk', qb, kk).astype(jnp.float32) * (HEAD_DIM ** -0.5)
    qi = jnp.arange(WINDOW)[:, None]
    kj = jnp.arange(2 * WINDOW)[None, :]
    band = (kj >= qi) & (kj <= qi + WINDOW)
    not_first = jnp.arange(nb)[:, None, None] > 0
    valid = band[None] & (not_first | (kj >= WINDOW)[None])
    s = jnp.where(valid[None, :, None, None], s, -jnp.inf)
    sink = sinks.reshape(B_KV_HEADS, B_GROUP).astype(jnp.float32)[None, None, :, :, None, None]
    p = sink_softmax(s, sink)
    o = jnp.einsum('bnhgqk,bnkhd->bnqhgd', p.astype(vv.dtype), vv).reshape(n, s_len, B_HEADS * dh)
    cw = min(WINDOW, s_len)
    return o, k[:, s_len - cw:], v[:, s_len - cw:]


def swa_sample(q, k, v, cache_k, cache_v, sinks):
    n, t_len, _, dh = q.shape
    cw = cache_k.shape[1]
    kk = jnp.concatenate([cache_k.astype(k.dtype), k], axis=1)
    vv = jnp.concatenate([cache_v.astype(v.dtype), v], axis=1)
    qg = q.reshape(n, t_len, B_KV_HEADS, B_GROUP, dh)
    s = jnp.einsum('bqhgd,bkhd->bhgqk', qg, kk).astype(jnp.float32) * (HEAD_DIM ** -0.5)
    q_pos = PAST_LEN + jnp.arange(t_len)
    k_pos = jnp.concatenate([PAST_LEN - cw + jnp.arange(cw), PAST_LEN + jnp.arange(t_len)])
    diff = q_pos[:, None] - k_pos[None, :]
    valid = (diff >= 0) & (diff <= WINDOW)
    s = jnp.where(valid[None, None, None], s, -jnp.inf)
    sink = sinks.reshape(B_KV_HEADS, B_GROUP).astype(jnp.float32)[None, :, :, None, None]
    p = sink_softmax(s, sink)
    o = jnp.einsum('bhgqk,bkhd->bqhgd', p.astype(vv.dtype), vv).reshape(n, t_len, B_HEADS * dh)
    return o, k, v


def hier_route(h, w_rg, b_rg, w_re, b_re):
    lg = (h @ w_rg).astype(jnp.float32) + b_rg.astype(jnp.float32)
    pg = jax.nn.softmax(lg, axis=-1)
    g_sel = jnp.argmax(lg, axis=-1)
    p_top = jnp.take_along_axis(pg, g_sel[:, None], axis=-1)
    le = ((h @ w_re).astype(jnp.float32) + b_re.astype(jnp.float32)).reshape(-1, N_GROUPS, EXPERTS_PER_GROUP)
    le_g = jnp.take_along_axis(le, g_sel[:, None, None], axis=1)[:, 0]
    top_v, top_i = lax.top_k(le_g, TOP_K)
    wts = p_top * jax.nn.softmax(top_v, axis=-1)
    idx = (g_sel[:, None] * EXPERTS_PER_GROUP + top_i).astype(jnp.int32)
    return idx, wts


def routed_experts(h, idx, wts, w_gate, w_up, w_down):
    n, d = h.shape
    a = n * TOP_K
    flat_e = idx.reshape(-1)
    order = jnp.argsort(flat_e).astype(jnp.int32)
    sorted_e = flat_e[order]
    counts = jnp.zeros((N_EXPERTS,), jnp.int32).at[flat_e].add(1)
    padded = (counts + MOE_BLOCK - 1) // MOE_BLOCK * MOE_BLOCK
    pad_end = jnp.cumsum(padded)
    pad_start = pad_end - padded
    seg_start = jnp.cumsum(counts) - counts
    rank = jnp.arange(a, dtype=jnp.int32) - seg_start[sorted_e]
    dest = pad_start[sorted_e] + rank
    n_blocks = -(-a // MOE_BLOCK) + N_EXPERTS
    rows = n_blocks * MOE_BLOCK
    tok_sorted = order // TOP_K
    row_token = jnp.full((rows,), n, jnp.int32).at[dest].set(tok_sorted)
    block_expert = jnp.minimum(jnp.searchsorted(pad_end, jnp.arange(n_blocks, dtype=jnp.int32) * MOE_BLOCK, side='right'), N_EXPERTS - 1)
    h_pad = jnp.concatenate([h, jnp.zeros((1, d), h.dtype)], axis=0)
    xb = h_pad[row_token].reshape(n_blocks, MOE_BLOCK, d)

    def one_group(args):
        xg, e = args
        return (jax.nn.silu(xg @ w_gate[e]) * (xg @ w_up[e])) @ w_down[e]

    yb = lax.map(one_group, (xb, block_expert)).reshape(rows, d)
    y_sorted = yb[dest] * wts.reshape(-1)[order][:, None].astype(h.dtype)
    return jax.ops.segment_sum(y_sorted, tok_sorted, num_segments=n)


def layer_forward(x, c, positions, attn_fn, w_ada, b_ada, norm1_g, w_in, ln_v_g, ln_v_b, w_s, b_s, w_pa, w_pb, w_o, norm2_g, w_rg, b_rg, w_re, b_re, w_gate, w_up, w_down):
    n, l, d = x.shape
    shift1, scale1, gate1, shift2, scale2, gate2 = adaln_params(c, w_ada, b_ada)
    h = rms_norm(x, norm1_g) * (1 + scale1) + shift1
    u, v_a, q, k, v_b, gates = jnp.split(h @ w_in, PROJ_SPLITS, axis=-1)
    u = jax.nn.gelu(u)
    v_n = layer_norm(jax.nn.gelu(v_a), ln_v_g, ln_v_b)
    out_a = chunk_spatial_gate(u, v_n, w_s, b_s)
    q = rope(q.reshape(n, l, B_HEADS, HEAD_DIM), positions)
    k = rope(k.reshape(n, l, B_KV_HEADS, HEAD_DIM), positions)
    v_b = v_b.reshape(n, l, B_KV_HEADS, HEAD_DIM)
    out_b, k_rows, v_rows = attn_fn(q, k, v_b)
    g = jax.nn.sigmoid(gates.astype(jnp.float32)).astype(x.dtype)
    g_a, g_b = jnp.split(g, N_BRANCH, axis=-1)
    merged = g_a * (out_a @ w_pa) + g_b * (out_b @ w_pb)
    x = x + gate1 * (merged @ w_o)
    h2 = (rms_norm(x, norm2_g) * (1 + scale2) + shift2).reshape(n * l, d)
    idx, wts = hier_route(h2, w_rg, b_rg, w_re, b_re)
    y = routed_experts(h2, idx, wts, w_gate, w_up, w_down).reshape(n, l, d)
    x = x + gate2 * y
    return x, k_rows, v_rows, v_n


def setup_inputs(seed: int = 0) -> dict:
    key = jax.random.key(seed)
    ks = jax.random.split(key, 32)
    f32 = jnp.float32

    def nrm(k, shape, scale=1.0):
        return jax.random.normal(k, shape, f32) * scale

    L = DEPTH
    D = D_MODEL
    cw = min(WINDOW, PAST_LEN)
    return {
        'x_prompt': nrm(ks[0], (BATCH, SEQ, D)),
        'x_sample': nrm(ks[1], (DEC_BATCH, DEC_SEQ, D)),
        'cache_k': nrm(ks[2], (L, DEC_BATCH, cw, B_KV_HEADS, HEAD_DIM)),
        'cache_v': nrm(ks[3], (L, DEC_BATCH, cw, B_KV_HEADS, HEAD_DIM)),
        'c_prompt': nrm(ks[4], (BATCH, D)),
        'c_sample': nrm(ks[5], (DEC_BATCH, D)),
        'w_ada': nrm(ks[6], (L, D, 6 * D), 0.5 * D ** -0.5),
        'b_ada': nrm(ks[7], (L, 6 * D), 0.02),
        'norm1_g': 1.0 + nrm(ks[8], (L, D), 0.02),
        'w_in': nrm(ks[9], (L, D, PROJ_COLS), D ** -0.5),
        'ln_v_g': 1.0 + nrm(ks[10], (L, A_WIDTH), 0.02),
        'ln_v_b': nrm(ks[11], (L, A_WIDTH), 0.02),
        'w_s': nrm(ks[12], (L, A_GROUPS, CHUNK, CHUNK), CHUNK ** -0.5),
        'b_s': 1.0 + nrm(ks[13], (L, A_GROUPS, CHUNK), 0.02),
        'attn_sinks': nrm(ks[14], (L, B_HEADS), 0.5),
        'w_pa': nrm(ks[15], (L, A_WIDTH, D), A_WIDTH ** -0.5),
        'w_pb': nrm(ks[16], (L, Q_WIDTH, D), Q_WIDTH ** -0.5),
        'w_o': nrm(ks[17], (L, D, D), D ** -0.5),
        'norm2_g': 1.0 + nrm(ks[18], (L, D), 0.02),
        'w_rg': nrm(ks[19], (L, D, N_GROUPS), D ** -0.5),
        'b_rg': nrm(ks[20], (L, N_GROUPS), 0.01),
        'w_re': nrm(ks[21], (L, D, N_EXPERTS), D ** -0.5),
        'b_re': nrm(ks[22], (L, N_EXPERTS), 0.01),
        'w_gate': nrm(ks[23], (L, N_EXPERTS, D, EXPERT_HIDDEN), D ** -0.5),
        'w_up': nrm(ks[24], (L, N_EXPERTS, D, EXPERT_HIDDEN), D ** -0.5),
        'w_down': nrm(ks[25], (L, N_EXPERTS, EXPERT_HIDDEN, D), EXPERT_HIDDEN ** -0.5),
        'normf_g': 1.0 + nrm(ks[26], (D,), 0.02),
    }


def reference(x_prompt, x_sample, cache_k, cache_v, c_prompt, c_sample, w_ada, b_ada, norm1_g, w_in, ln_v_g, ln_v_b, w_s, b_s, attn_sinks, w_pa, w_pb, w_o, norm2_g, w_rg, b_rg, w_re, b_re, w_gate, w_up, w_down, normf_g):
    xp, xs = x_prompt, x_sample
    pos_p = jnp.arange(xp.shape[1])
    pos_s = PAST_LEN + jnp.arange(xs.shape[1])
    kp_list, vp_list, ks_list, vs_list, cv_list = [], [], [], [], []
    for l in range(DEPTH):
        lw = (w_ada[l], b_ada[l], norm1_g[l], w_in[l], ln_v_g[l], ln_v_b[l], w_s[l], b_s[l], w_pa[l], w_pb[l], w_o[l], norm2_g[l], w_rg[l], b_rg[l], w_re[l], b_re[l], w_gate[l], w_up[l], w_down[l])
        prompt_attn = functools.partial(swa_prompt, sinks=attn_sinks[l])
        sample_attn = functools.partial(swa_sample, cache_k=cache_k[l], cache_v=cache_v[l], sinks=attn_sinks[l])
        xp, kp, vp, _ = layer_forward(xp, c_prompt, pos_p, prompt_attn, *lw)
        xs, k_s, v_s, cv_s = layer_forward(xs, c_sample, pos_s, sample_attn, *lw)
        kp_list.append(kp)
        vp_list.append(vp)
        ks_list.append(k_s)
        vs_list.append(v_s)
        cv_list.append(cv_s)
    y_prompt = rms_norm(xp, normf_g)
    y_sample = rms_norm(xs, normf_g)
    k_prompt = jnp.stack(kp_list)
    v_prompt = jnp.stack(vp_list)
    k_sample = jnp.stack(ks_list)
    v_sample = jnp.stack(vs_list)
    chunk_v_sample = jnp.stack(cv_list)
    return (y_prompt, y_sample, k_prompt, v_prompt, k_sample, v_sample, chunk_v_sample)
```

```python
import functools

import jax
import jax.numpy as jnp
import numpy as np
from jax import lax
from jax.experimental import pallas as pl
from jax.experimental.pallas import tpu as pltpu

D_MODEL = 1024
A_WIDTH = 512
A_GROUPS = 4
A_GROUP_DIM = 128
CHUNK = 128
HEAD_DIM = 64
B_HEADS = 8
B_KV_HEADS = 2
Q_WIDTH = 512
KV_WIDTH = 128
WINDOW = 128
ROPE_THETA = 10000.0
N_GROUPS = 4
EXPERTS_PER_GROUP = 8
N_EXPERTS = 32
TOP_K = 2
EXPERT_HIDDEN = 256
EPS = 1e-6
PAST_LEN = 8192

C_U = 0
C_V = A_WIDTH
C_Q = 2 * A_WIDTH
C_K = C_Q + Q_WIDTH
C_VB = C_K + KV_WIDTH
C_G = C_VB + KV_WIDTH
PROJ_COLS = C_G + 2 * D_MODEL

LANE = 128
ROW_TILE = 256
MOE_BLOCK = 128
ROUTE_COLS = 128
NEG_BIG = -1e30
VMEM_LIMIT = 56 * 1024 * 1024

F32 = jnp.float32
BF16 = jnp.bfloat16


def _dot(a, b):
    return jnp.dot(a, b, preferred_element_type=F32)


def _dot_nt(a, b):
    return lax.dot_general(a, b, (((1,), (1,)), ((), ())), preferred_element_type=F32)


def _rms(x, g):
    return x * lax.rsqrt(jnp.mean(x * x, axis=-1, keepdims=True) + EPS) * g


def _gelu(x):
    return x * (0.5 * (1.0 + jnp.tanh(0.7978845608028654 * (x + 0.044715 * (x * x * x)))))


def _sigmoid(x):
    return 1.0 / (1.0 + jnp.exp(-x))


def _rope(x, cos, sin_signed):
    lane = lax.broadcasted_iota(jnp.int32, cos.shape, 1)
    first = (lane % HEAD_DIM) < (HEAD_DIM // 2)
    outs = []
    for c in range(x.shape[1] // LANE):
        xc = x[:, c * LANE:(c + 1) * LANE]
        rot = jnp.where(first, pltpu.roll(xc, LANE - HEAD_DIM // 2, 1), pltpu.roll(xc, HEAD_DIM // 2, 1))
        outs.append(xc * cos + rot * sin_signed)
    return outs[0] if len(outs) == 1 else jnp.concatenate(outs, axis=1)


def _front(x, mod_ref, g1, w_in_ref, lnv_g, lnv_b, ws_ref, bcol, cos, sin_signed, chunk):
    t = x.shape[0]
    h = (_rms(x, g1) * (1.0 + mod_ref[1]) + mod_ref[0]).astype(BF16)
    u = _gelu(_dot(h, w_in_ref[:, C_U:C_V]))
    va = _gelu(_dot(h, w_in_ref[:, C_V:C_Q]))
    mu = jnp.mean(va, axis=-1, keepdims=True)
    vc = va - mu
    v_n = vc * lax.rsqrt(jnp.mean(vc * vc, axis=-1, keepdims=True) + EPS) * lnv_g + lnv_b
    r = lax.broadcasted_iota(jnp.int32, (t, t), 0)
    c = lax.broadcasted_iota(jnp.int32, (t, t), 1)
    tri = ((r // chunk) == (c // chunk)) & ((c % chunk) <= (r % chunk))
    v_bf = v_n.astype(BF16)
    mixed = []
    for g in range(A_GROUPS):
        wg = jnp.where(tri, ws_ref[g], 0.0).astype(BF16)
        mixed.append(_dot(wg, v_bf[:, g * A_GROUP_DIM:(g + 1) * A_GROUP_DIM]) + bcol[:, g:g + 1])
    out_a = u * jnp.concatenate(mixed, axis=1)
    q = _rope(_dot(h, w_in_ref[:, C_Q:C_K]), cos, sin_signed) * (HEAD_DIM ** -0.5)
    k = _rope(_dot(h, w_in_ref[:, C_K:C_VB]), cos, sin_signed)
    v_b = _dot(h, w_in_ref[:, C_VB:C_G])
    gpre = _dot(h, w_in_ref[:, C_G:PROJ_COLS])
    return out_a, v_n, q, k, v_b, gpre


def _route(h2_bf, w_r_ref, b_r, cnt_ref, ri_ref, rf_ref):
    t = h2_bf.shape[0]
    lt = (_dot(h2_bf, w_r_ref[...]) + b_r).T
    lg = lt[0:8]
    le = lt[8:8 + N_EXPERTS]
    i8 = lax.broadcasted_iota(jnp.int32, (8, t), 0).astype(F32)
    gmax = jnp.max(lg, axis=0, keepdims=True)
    g_sel = jnp.min(jnp.where(lg == gmax, i8, 8.0), axis=0, keepdims=True)
    p_top = 1.0 / jnp.sum(jnp.exp(lg - gmax), axis=0, keepdims=True)
    le_g = jnp.zeros((8, t), F32)
    for g in range(N_GROUPS):
        le_g = jnp.where(g_sel == float(g), le[g * 8:(g + 1) * 8], le_g)
    v1 = jnp.max(le_g, axis=0, keepdims=True)
    i1 = jnp.min(jnp.where(le_g == v1, i8, 8.0), axis=0, keepdims=True)
    rest = jnp.where(i8 == i1, -jnp.inf, le_g)
    v2 = jnp.max(rest, axis=0, keepdims=True)
    i2 = jnp.min(jnp.where(rest == v2, i8, 8.0), axis=0, keepdims=True)
    e2 = jnp.exp(v2 - v1)
    w1 = p_top * (1.0 / (1.0 + e2))
    w2 = p_top * (e2 / (1.0 + e2))
    e0 = g_sel * 8.0 + i1
    e1 = g_sel * 8.0 + i2
    i32 = lax.broadcasted_iota(jnp.int32, (N_EXPERTS, t), 0).astype(F32)
    hit0 = i32 == e0
    hit1 = i32 == e1
    onehot = jnp.where(hit0 | hit1, 1.0, 0.0)
    r = lax.broadcasted_iota(jnp.int32, (t, t), 0)
    c = lax.broadcasted_iota(jnp.int32, (t, t), 1)
    upper = jnp.where(r < c, 1.0, 0.0).astype(BF16)
    before = _dot(onehot.astype(BF16), upper) + cnt_ref[...]
    rank0 = jnp.sum(jnp.where(hit0, before, 0.0), axis=0, keepdims=True)
    rank1 = jnp.sum(jnp.where(hit1, before, 0.0), axis=0, keepdims=True)
    cnt_ref[...] = cnt_ref[...] + jnp.sum(onehot, axis=1, keepdims=True)
    ri = jnp.where(i8 == 0.0, e0, jnp.where(i8 == 1.0, e1, jnp.where(i8 == 2.0, rank0, jnp.where(i8 == 3.0, rank1, 0.0))))
    ri_ref[...] = ri.astype(jnp.int32)
    rf_ref[...] = jnp.where(i8 == 0.0, w1, jnp.where(i8 == 1.0, w2, 0.0))


def _back(x, mod_ref, out_a, out_b, gpre, w_pa_ref, w_pb_ref, w_o_ref, g2, w_r_ref, b_r, cnt_ref,
          x1_ref, h2_ref, ri_ref, rf_ref):
    gates = _sigmoid(gpre)
    pa = _dot(out_a.astype(BF16), w_pa_ref[...])
    pb = _dot(out_b.astype(BF16), w_pb_ref[...])
    merged = gates[:, :D_MODEL] * pa + gates[:, D_MODEL:] * pb
    x1 = x + mod_ref[2] * _dot(merged.astype(BF16), w_o_ref[...])
    x1_ref[...] = x1
    h2 = _rms(x1, g2) * (1.0 + mod_ref[4]) + mod_ref[3]
    h2_ref[...] = h2
    _route(h2.astype(BF16), w_r_ref, b_r, cnt_ref, ri_ref, rf_ref)


def _window_attention(q, k2, v2, notfirst, sinks_ref):
    lane = lax.broadcasted_iota(jnp.int32, k2.shape, 1)
    lo = lane < HEAD_DIM
    k2r = pltpu.roll(k2, HEAD_DIM, 1)
    v2r = pltpu.roll(v2, HEAD_DIM, 1)
    qi = lax.broadcasted_iota(jnp.int32, (WINDOW, 2 * WINDOW), 0)
    kj = lax.broadcasted_iota(jnp.int32, (WINDOW, 2 * WINDOW), 1)
    valid = (kj >= qi) & (kj <= qi + WINDOW) & ((kj >= WINDOW) | notfirst)
    q_bf = q.astype(BF16)
    outs = []
    for kvh in range(B_KV_HEADS):
        ka, kb = (k2, k2r) if kvh == 0 else (k2r, k2)
        va, vb = (v2, v2r) if kvh == 0 else (v2r, v2)
        kcat = jnp.concatenate([jnp.where(lo, ka, 0.0), jnp.where(lo, 0.0, kb)], axis=0).astype(BF16)
        vcat = jnp.concatenate([jnp.where(lo, va, 0.0), jnp.where(lo, 0.0, vb)], axis=0).astype(BF16)
        qst = jnp.concatenate([q_bf[:, kvh * 256:kvh * 256 + LANE], q_bf[:, kvh * 256 + LANE:kvh * 256 + 2 * LANE]], axis=0)
        s = _dot_nt(qst, kcat)
        prows = []
        for pr in range(2):
            pcols = []
            for eo in range(2):
                sink = sinks_ref[kvh * 4 + pr * 2 + eo]
                sq = jnp.where(valid, s[pr * WINDOW:(pr + 1) * WINDOW, eo * 256:(eo + 1) * 256], -jnp.inf)
                m = jnp.maximum(jnp.max(sq, axis=-1, keepdims=True), sink)
                e = jnp.exp(sq - m)
                z = jnp.sum(e, axis=-1, keepdims=True) + jnp.exp(sink - m)
                pcols.append((e / z).astype(BF16))
            prows.append(jnp.concatenate(pcols, axis=1))
        o = _dot(jnp.concatenate(prows, axis=0), vcat)
        outs.append(o[0:WINDOW])
        outs.append(o[WINDOW:2 * WINDOW])
    return jnp.concatenate(outs, axis=1)


def _ada_kernel(c_ref, w_ref, b_ref, o_ref):
    c = c_ref[...]
    s = (c * _sigmoid(c)).astype(BF16)
    o_ref[...] = _dot(s, w_ref[...].astype(BF16)) + b_ref[...]


def _mix_prompt_kernel(sinks_ref, x_ref, mod_ref, cos_ref, sin_ref, g1_ref, w_in_ref, lnvg_ref, lnvb_ref, ws_ref,
                       bcol_ref, w_pa_ref, w_pb_ref, w_o_ref, g2_ref, w_r_ref, b_r_ref, cnt_in_ref,
                       x1_ref, h2_ref, ri_ref, rf_ref, klast_ref, vlast_ref, cnt_out_ref,
                       kc_ref, vc_ref, cnt_ref):
    b = pl.program_id(0)
    j = pl.program_id(1)

    @pl.when(j == 0)
    def _():
        kc_ref[...] = jnp.zeros_like(kc_ref)
        vc_ref[...] = jnp.zeros_like(vc_ref)

    @pl.when((b == 0) & (j == 0))
    def _():
        cnt_ref[...] = cnt_in_ref[...]

    x = x_ref[...]
    out_a, _, q, k, v_b, gpre = _front(x, mod_ref, g1_ref[...], w_in_ref, lnvg_ref[...], lnvb_ref[...], ws_ref,
                                       bcol_ref[...], cos_ref[...], sin_ref[...], CHUNK)
    obs = []
    for cb in range(ROW_TILE // WINDOW):
        rows = slice(cb * WINDOW, (cb + 1) * WINDOW)
        if cb == 0:
            kprev, vprev, notfirst = kc_ref[...], vc_ref[...], j > 0
        else:
            prev = slice((cb - 1) * WINDOW, cb * WINDOW)
            kprev, vprev, notfirst = k[prev], v_b[prev], True
        k2 = jnp.concatenate([kprev, k[rows]], axis=0)
        v2 = jnp.concatenate([vprev, v_b[rows]], axis=0)
        obs.append(_window_attention(q[rows], k2, v2, notfirst, sinks_ref))
    out_b = jnp.concatenate(obs, axis=0)
    last = slice(ROW_TILE - WINDOW, ROW_TILE)
    kc_ref[...] = k[last]
    vc_ref[...] = v_b[last]

    @pl.when(j == pl.num_programs(1) - 1)
    def _():
        klast_ref[...] = k[last]
        vlast_ref[...] = v_b[last]

    _back(x, mod_ref, out_a, out_b, gpre, w_pa_ref, w_pb_ref, w_o_ref, g2_ref[...], w_r_ref, b_r_ref[...], cnt_ref,
          x1_ref, h2_ref, ri_ref, rf_ref)
    cnt_out_ref[...] = cnt_ref[...]


def _front_sample_kernel(x_ref, mod_ref, cos_ref, sin_ref, g1_ref, w_in_ref, lnvg_ref, lnvb_ref, ws_ref, bcol_ref,
                         oa_ref, vn_ref, q_ref, k_ref, v_ref, gp_ref):
    out_a, v_n, q, k, v_b, gpre = _front(x_ref[...], mod_ref, g1_ref[...], w_in_ref, lnvg_ref[...], lnvb_ref[...],
                                         ws_ref, bcol_ref[...], cos_ref[...], sin_ref[...], 4)
    oa_ref[...] = out_a
    vn_ref[...] = v_n
    q_ref[...] = q
    k_ref[...] = k
    v_ref[...] = v_b
    gp_ref[...] = gpre


def _attn_sample_kernel(q_ref, kn_ref, vn_ref, ck_ref, cv_ref, sink_ref, o_ref):
    bmm_nt = lambda a, b: lax.dot_general(a, b, (((2,), (2,)), ((0,), (0,))), preferred_element_type=F32)
    bmm = lambda a, b: lax.dot_general(a, b, (((2,), (1,)), ((0,), (0,))), preferred_element_type=F32)
    q = q_ref[...].astype(BF16)
    nrow = q.shape[1]
    s_c = bmm_nt(q, ck_ref[...].astype(BF16))
    s_n = bmm_nt(q, kn_ref[...].astype(BF16))
    pos_c = (lax.broadcasted_iota(jnp.int32, s_c.shape, 1) % 16) // 4
    pos_n = (lax.broadcasted_iota(jnp.int32, s_n.shape, 1) % 16) // 4
    s_c = jnp.where(lax.broadcasted_iota(jnp.int32, s_c.shape, 2) >= pos_c, s_c, -jnp.inf)
    s_n = jnp.where(lax.broadcasted_iota(jnp.int32, s_n.shape, 2) <= pos_n, s_n, -jnp.inf)
    sink = sink_ref[...][None]
    m = jnp.maximum(jnp.maximum(jnp.max(s_c, axis=-1, keepdims=True), jnp.max(s_n, axis=-1, keepdims=True)), sink)
    e_c = jnp.exp(s_c - m)
    e_n = jnp.exp(s_n - m)
    z = jnp.sum(e_c, axis=-1, keepdims=True) + jnp.sum(e_n, axis=-1, keepdims=True) + jnp.exp(sink - m)
    o = bmm((e_c / z).astype(BF16), cv_ref[...].astype(BF16)) + bmm((e_n / z).astype(BF16), vn_ref[...].astype(BF16))
    row = lax.broadcasted_iota(jnp.int32, o.shape, 1)
    lane = lax.broadcasted_iota(jnp.int32, o.shape, 2)
    o_ref[...] = jnp.where((row < nrow // 2) == (lane < HEAD_DIM), o, 0.0)


def _back_sample_kernel(x_ref, mod_ref, oa_ref, ob_ref, gp_ref, w_pa_ref, w_pb_ref, w_o_ref, g2_ref, w_r_ref, b_r_ref,
                        cnt_in_ref, x1_ref, h2_ref, ri_ref, rf_ref, cnt_out_ref, cnt_ref):
    @pl.when(pl.program_id(0) == 0)
    def _():
        cnt_ref[...] = cnt_in_ref[...]

    _back(x_ref[...], mod_ref, oa_ref[...], ob_ref[...], gp_ref[...], w_pa_ref, w_pb_ref, w_o_ref, g2_ref[...],
          w_r_ref, b_r_ref[...], cnt_ref, x1_ref, h2_ref, ri_ref, rf_ref)
    cnt_out_ref[...] = cnt_ref[...]


def _dispatch_kernel(d0_ref, d1_ref, h_ref, xs_in_ref, xs_ref, sem):
    del xs_in_ref
    base = pl.program_id(0) * ROW_TILE

    def row_copy(t, dst):
        return pltpu.make_async_copy(h_ref.at[pl.ds(t, 1)], xs_ref.at[pl.ds(dst, 1)], sem)

    def issue(t, carry):
        row_copy(t, d0_ref[base + t]).start()
        row_copy(t, d1_ref[base + t]).start()
        return carry

    lax.fori_loop(0, ROW_TILE, issue, 0, unroll=8)

    def drain(t, carry):
        row_copy(0, 0).wait()
        row_copy(0, 0).wait()
        return carry

    lax.fori_loop(0, ROW_TILE, drain, 0, unroll=8)


def _moe_kernel(be_ref, nb_ref, xs_ref, wg_ref, wu_ref, wd_ref, yb_ref, wg_s, wu_s, wd_s):
    i = pl.program_id(0)

    @pl.when(i < nb_ref[0])
    def _():
        changed = (i == 0) | (be_ref[i] != be_ref[jnp.maximum(i - 1, 0)])

        @pl.when(changed)
        def _():
            wg_s[...] = wg_ref[...].astype(BF16)
            wu_s[...] = wu_ref[...].astype(BF16)
            wd_s[...] = wd_ref[...].astype(BF16)

        x = xs_ref[...].astype(BF16)
        g = _dot(x, wg_s[...])
        u = _dot(x, wu_s[...])
        hmid = (g * _sigmoid(g)) * u
        yb_ref[...] = _dot(hmid.astype(BF16), wd_s[...])

    @pl.when(i >= nb_ref[0])
    def _():
        yb_ref[...] = jnp.zeros_like(yb_ref)


def _combine_kernel(d0_ref, d1_ref, x1_ref, gate_ref, w_ref, gf_ref, yb_ref, y_ref, buf, sem):
    i = pl.program_id(0)
    n = pl.num_programs(0)

    def row_copy(slot, kk, t, src):
        return pltpu.make_async_copy(yb_ref.at[pl.ds(src, 1)], buf.at[slot, kk, pl.ds(t, 1)], sem.at[slot])

    def issue_step(step, slot):
        base = step * ROW_TILE

        def issue(t, carry):
            row_copy(slot, 0, t, d0_ref[base + t]).start()
            row_copy(slot, 1, t, d1_ref[base + t]).start()
            return carry

        lax.fori_loop(0, ROW_TILE, issue, 0, unroll=8)

    slot = i % 2

    @pl.when(i == 0)
    def _():
        issue_step(0, 0)

    @pl.when(i + 1 < n)
    def _():
        issue_step(i + 1, 1 - slot)

    def drain(t, carry):
        row_copy(slot, 0, 0, 0).wait()
        row_copy(slot, 1, 0, 0).wait()
        return carry

    lax.fori_loop(0, ROW_TILE, drain, 0, unroll=8)

    w = w_ref[...]
    y = w[:, 0:1] * buf[slot, 0] + w[:, 1:2] * buf[slot, 1]
    x2 = x1_ref[...] + gate_ref[...] * y
    y_ref[...] = _rms(x2, gf_ref[...])


def _const_spec(shape):
    nd = len(shape)
    return pl.BlockSpec(shape, lambda *_: (0,) * nd, pipeline_mode=pl.Buffered(1))


def _params(n_axes):
    return pltpu.CompilerParams(dimension_semantics=("arbitrary",) * n_axes, vmem_limit_bytes=VMEM_LIMIT)


def _rope_tables(positions):
    half = HEAD_DIM // 2
    inv = jnp.power(jnp.float32(ROPE_THETA), -2.0 * jnp.arange(half, dtype=jnp.float32) / HEAD_DIM)
    ang = positions.astype(jnp.float32)[:, None] * inv[None, :]
    cos = jnp.cos(ang)
    sin = jnp.sin(ang)
    return jnp.concatenate([cos, cos, cos, cos], axis=1), jnp.concatenate([-sin, sin, -sin, sin], axis=1)


def _adaln(c_all, w_ada, b_ada):
    rows = c_all.shape[0]
    ncol = w_ada.shape[1]
    bn = 1536
    return pl.pallas_call(
        _ada_kernel,
        grid=(ncol // bn,),
        in_specs=[pl.BlockSpec((rows, D_MODEL), lambda i: (0, 0)),
                  pl.BlockSpec((D_MODEL, bn), lambda i: (0, i)),
                  pl.BlockSpec((1, bn), lambda i: (0, i))],
        out_specs=pl.BlockSpec((rows, bn), lambda i: (0, i)),
        out_shape=jax.ShapeDtypeStruct((rows, ncol), F32),
        compiler_params=_params(1),
        name="adaln",
    )(c_all, w_ada, b_ada.reshape(1, ncol))


def _route_outputs(n_tok):
    shapes = [jax.ShapeDtypeStruct((n_tok, D_MODEL), F32),
              jax.ShapeDtypeStruct((n_tok, D_MODEL), F32),
              jax.ShapeDtypeStruct((8, n_tok), jnp.int32),
              jax.ShapeDtypeStruct((8, n_tok), F32)]
    return shapes


def kernel(x_prompt, x_sample, cache_k, cache_v, c_prompt, c_sample, w_ada, b_ada, norm1_g, w_in, ln_v_g, ln_v_b, w_s, b_s, attn_sinks, w_pa, w_pb, w_o, norm2_g, w_rg, b_rg, w_re, b_re, w_gate, w_up, w_down, normf_g):
    nb, seq, d = x_prompt.shape
    ns, dec = x_sample.shape[0], x_sample.shape[1]
    n_p = nb * seq
    n_s = ns * dec
    T = ROW_TILE

    w_in_bf = w_in[0].astype(BF16)
    w_pa_bf = w_pa[0].astype(BF16)
    w_pb_bf = w_pb[0].astype(BF16)
    w_o_bf = w_o[0].astype(BF16)
    g1 = norm1_g[0].reshape(1, d)
    g2 = norm2_g[0].reshape(1, d)
    gf = normf_g.reshape(1, d)
    lnvg = ln_v_g[0].reshape(1, A_WIDTH)
    lnvb = ln_v_b[0].reshape(1, A_WIDTH)
    sinks = attn_sinks[0]
    w_r = jnp.zeros((d, ROUTE_COLS), F32).at[:, 0:N_GROUPS].set(w_rg[0]).at[:, 8:8 + N_EXPERTS].set(w_re[0]).astype(BF16)
    b_r = jnp.zeros((ROUTE_COLS,), F32).at[0:N_GROUPS].set(b_rg[0]).at[N_GROUPS:8].set(NEG_BIG).at[8:8 + N_EXPERTS].set(b_re[0]).reshape(1, ROUTE_COLS)
    ws_p = jnp.tile(w_s[0], (1, T // CHUNK, T // CHUNK))
    bcol_p = jnp.tile(b_s[0], (1, T // CHUNK)).T
    ws_s = jnp.tile(w_s[0][:, :dec, :dec], (1, T // dec, T // dec))
    bcol_s = jnp.tile(b_s[0][:, :dec], (1, T // dec)).T

    n_c = nb + ns
    rows_c = -(-n_c // 16) * 16
    c_all = jnp.concatenate([c_prompt, c_sample, jnp.zeros((rows_c - n_c, d), F32)], axis=0)
    m_all = _adaln(c_all, w_ada[0], b_ada[0])
    mod_p = m_all[0:nb].reshape(nb, 6, 1, d)
    mod_s = jnp.repeat(m_all[nb:n_c].reshape(ns, 6, d), dec, axis=0).transpose(1, 0, 2)

    cos_p, sin_p = _rope_tables(jnp.arange(seq))
    cos_s, sin_s = _rope_tables(PAST_LEN + (jnp.arange(n_s) % dec))

    cnt0 = jnp.zeros((N_EXPERTS, T), F32)
    weight_specs = [_const_spec((d, D_MODEL)), _const_spec((A_WIDTH, d)), _const_spec((d, d))]

    steps = seq // T
    tok = lambda b, j, *_: (b * steps + j, 0)
    x1_p, h2_p, ri_p, rf_p, k_last, v_last, cnt_p = pl.pallas_call(
        _mix_prompt_kernel,
        grid_spec=pltpu.PrefetchScalarGridSpec(
            num_scalar_prefetch=0,
            grid=(nb, steps),
            in_specs=[pl.BlockSpec(memory_space=pltpu.SMEM),
                      pl.BlockSpec((T, d), tok),
                      pl.BlockSpec((None, 6, 1, d), lambda b, j: (b, 0, 0, 0)),
                      pl.BlockSpec((T, LANE), lambda b, j: (j, 0)),
                      pl.BlockSpec((T, LANE), lambda b, j: (j, 0)),
                      _const_spec((1, d)),
                      _const_spec((d, PROJ_COLS)),
                      _const_spec((1, A_WIDTH)),
                      _const_spec((1, A_WIDTH)),
                      _const_spec((A_GROUPS, T, T)),
                      _const_spec((T, A_GROUPS)),
                      _const_spec((A_WIDTH, d)),
                      _const_spec((Q_WIDTH, d)),
                      _const_spec((d, d)),
                      _const_spec((1, d)),
                      _const_spec((d, ROUTE_COLS)),
                      _const_spec((1, ROUTE_COLS)),
                      _const_spec((N_EXPERTS, T))],
            out_specs=[pl.BlockSpec((T, d), tok),
                       pl.BlockSpec((T, d), tok),
                       pl.BlockSpec((8, T), lambda b, j: (0, b * steps + j)),
                       pl.BlockSpec((8, T), lambda b, j: (0, b * steps + j)),
                       pl.BlockSpec((None, WINDOW, KV_WIDTH), lambda b, j: (b, 0, 0)),
                       pl.BlockSpec((None, WINDOW, KV_WIDTH), lambda b, j: (b, 0, 0)),
                       pl.BlockSpec((N_EXPERTS, T), lambda b, j: (0, 0))],
            scratch_shapes=[pltpu.VMEM((WINDOW, KV_WIDTH), F32),
                            pltpu.VMEM((WINDOW, KV_WIDTH), F32),
                            pltpu.VMEM((N_EXPERTS, T), F32)]),
        out_shape=_route_outputs(n_p) + [jax.ShapeDtypeStruct((nb, WINDOW, KV_WIDTH), F32),
                                         jax.ShapeDtypeStruct((nb, WINDOW, KV_WIDTH), F32),
                                         jax.ShapeDtypeStruct((N_EXPERTS, T), F32)],
        compiler_params=_params(2),
        name="mix_prompt",
    )(sinks, x_prompt.reshape(n_p, d), mod_p, cos_p, sin_p, g1, w_in_bf, lnvg, lnvb, ws_p, bcol_p,
      w_pa_bf, w_pb_bf, w_o_bf, g2, w_r, b_r, cnt0)

    xs_flat = x_sample.reshape(n_s, d)
    row_blk = lambda w: pl.BlockSpec((T, w), lambda i: (i, 0))
    mod_blk = pl.BlockSpec((6, T, d), lambda i: (0, i, 0))
    oa_s, vn_s, q_s, k_s, v_s, gp_s = pl.pallas_call(
        _front_sample_kernel,
        grid=(n_s // T,),
        in_specs=[row_blk(d), mod_blk, row_blk(LANE), row_blk(LANE), _const_spec((1, d)), _const_spec((d, PROJ_COLS)),
                  _const_spec((1, A_WIDTH)), _const_spec((1, A_WIDTH)), _const_spec((A_GROUPS, T, T)),
                  _const_spec((T, A_GROUPS))],
        out_specs=[row_blk(A_WIDTH), row_blk(A_WIDTH), row_blk(Q_WIDTH), row_blk(KV_WIDTH), row_blk(KV_WIDTH),
                   row_blk(2 * d)],
        out_shape=[jax.ShapeDtypeStruct((n_s, A_WIDTH), F32), jax.ShapeDtypeStruct((n_s, A_WIDTH), F32),
                   jax.ShapeDtypeStruct((n_s, Q_WIDTH), F32), jax.ShapeDtypeStruct((n_s, KV_WIDTH), F32),
                   jax.ShapeDtypeStruct((n_s, KV_WIDTH), F32), jax.ShapeDtypeStruct((n_s, 2 * d), F32)],
        compiler_params=_params(1),
        name="front_sample",
    )(xs_flat, mod_s, cos_s, sin_s, g1, w_in_bf, lnvg, lnvb, ws_s, bcol_s)

    grp = B_HEADS // B_KV_HEADS
    nrow = B_KV_HEADS * dec * grp
    q5 = q_s.reshape(ns, dec, B_KV_HEADS, grp, HEAD_DIM).transpose(0, 2, 1, 3, 4).reshape(ns, B_KV_HEADS, dec * grp, HEAD_DIM)
    zq = jnp.zeros_like(q5[:, 0])
    q_pad = jnp.concatenate([jnp.concatenate([q5[:, 0], zq], axis=-1), jnp.concatenate([zq, q5[:, 1]], axis=-1)], axis=1)
    kn = jnp.pad(k_s.reshape(ns, dec, KV_WIDTH), ((0, 0), (0, 8 - dec), (0, 0)))
    vn = jnp.pad(v_s.reshape(ns, dec, KV_WIDTH), ((0, 0), (0, 8 - dec), (0, 0)))
    cw = cache_k.shape[2]
    ck = cache_k[0].reshape(ns, cw, KV_WIDTH)
    cv = cache_v[0].reshape(ns, cw, KV_WIDTH)
    rr = np.arange(nrow)
    sink_col = sinks[(rr // (dec * grp)) * grp + rr % grp].reshape(nrow, 1)
    bs = 16
    seq_blk = lambda r, w: pl.BlockSpec((bs, r, w), lambda i: (i, 0, 0))
    o_pad = pl.pallas_call(
        _attn_sample_kernel,
        grid=(ns // bs,),
        in_specs=[seq_blk(nrow, KV_WIDTH), seq_blk(8, KV_WIDTH), seq_blk(8, KV_WIDTH), seq_blk(cw, KV_WIDTH),
                  seq_blk(cw, KV_WIDTH), pl.BlockSpec((nrow, 1), lambda i: (0, 0))],
        out_specs=seq_blk(nrow, KV_WIDTH),
        out_shape=jax.ShapeDtypeStruct((ns, nrow, KV_WIDTH), F32),
        compiler_params=_params(1),
        name="attn_sample",
    )(q_pad, kn, vn, ck, cv, sink_col)
    half = nrow // 2
    o5 = jnp.stack([o_pad[:, :half, :HEAD_DIM], o_pad[:, half:, HEAD_DIM:]], axis=1)
    ob_s = o5.reshape(ns, B_KV_HEADS, dec, grp, HEAD_DIM).transpose(0, 2, 1, 3, 4).reshape(n_s, Q_WIDTH)

    x1_s, h2_s, ri_s, rf_s, cnt_all = pl.pallas_call(
        _back_sample_kernel,
        grid=(n_s // T,),
        in_specs=[row_blk(d), mod_blk, row_blk(A_WIDTH), row_blk(Q_WIDTH), row_blk(2 * d),
                  _const_spec((A_WIDTH, d)), _const_spec((Q_WIDTH, d)), _const_spec((d, d)), _const_spec((1, d)),
                  _const_spec((d, ROUTE_COLS)), _const_spec((1, ROUTE_COLS)), _const_spec((N_EXPERTS, T))],
        out_specs=[row_blk(d), row_blk(d), pl.BlockSpec((8, T), lambda i: (0, i)), pl.BlockSpec((8, T), lambda i: (0, i)),
                   pl.BlockSpec((N_EXPERTS, T), lambda i: (0, 0))],
        out_shape=_route_outputs(n_s) + [jax.ShapeDtypeStruct((N_EXPERTS, T), F32)],
        scratch_shapes=[pltpu.VMEM((N_EXPERTS, T), F32)],
        compiler_params=_params(1),
        name="back_sample",
    )(xs_flat, mod_s, oa_s, ob_s, gp_s, w_pa_bf, w_pb_bf, w_o_bf, g2, w_r, b_r, cnt_p)

    counts = cnt_all[:, 0].astype(jnp.int32)
    padded = (counts + MOE_BLOCK - 1) // MOE_BLOCK * MOE_BLOCK
    pad_end = jnp.cumsum(padded)
    pad_start = pad_end - padded
    n_blocks = -(-((n_p + n_s) * TOP_K) // MOE_BLOCK) + N_EXPERTS
    rows = n_blocks * MOE_BLOCK
    blk_start = jnp.arange(n_blocks, dtype=jnp.int32) * MOE_BLOCK
    block_expert = jnp.minimum(jnp.sum(pad_end[None, :] <= blk_start[:, None], axis=1), N_EXPERTS - 1).astype(jnp.int32)
    n_used = (pad_end[-1] // MOE_BLOCK).astype(jnp.int32).reshape(1)
    eids = jnp.arange(N_EXPERTS, dtype=jnp.int32)[:, None]

    def dests(ri):
        start = lambda e: jnp.sum(jnp.where(e[None, :] == eids, pad_start[:, None], 0), axis=0)
        return start(ri[0]) + ri[2], start(ri[1]) + ri[3]

    d0_p, d1_p = dests(ri_p)
    d0_s, d1_s = dests(ri_s)

    def dispatch(d0, d1, h2, xs):
        n_tok = h2.shape[0]
        return pl.pallas_call(
            _dispatch_kernel,
            grid_spec=pltpu.PrefetchScalarGridSpec(
                num_scalar_prefetch=2,
                grid=(n_tok // T,),
                in_specs=[pl.BlockSpec((T, d), lambda i, *_: (i, 0)), pl.BlockSpec(memory_space=pl.ANY)],
                out_specs=pl.BlockSpec(memory_space=pl.ANY),
                scratch_shapes=[pltpu.SemaphoreType.DMA]),
            out_shape=jax.ShapeDtypeStruct(xs.shape, F32),
            input_output_aliases={3: 0},
            compiler_params=_params(1),
            name="dispatch",
        )(d0, d1, h2, xs)

    xs = jnp.zeros((rows, d), F32)
    xs = dispatch(d0_p, d1_p, h2_p, xs)
    xs = dispatch(d0_s, d1_s, h2_s, xs)

    used_blk = lambda i, be, nu: (jnp.minimum(i, nu[0] - 1), 0)
    yb = pl.pallas_call(
        _moe_kernel,
        grid_spec=pltpu.PrefetchScalarGridSpec(
            num_scalar_prefetch=2,
            grid=(n_blocks,),
            in_specs=[pl.BlockSpec((MOE_BLOCK, d), used_blk),
                      pl.BlockSpec((None, d, EXPERT_HIDDEN), lambda i, be, nu: (be[i], 0, 0)),
                      pl.BlockSpec((None, d, EXPERT_HIDDEN), lambda i, be, nu: (be[i], 0, 0)),
                      pl.BlockSpec((None, EXPERT_HIDDEN, d), lambda i, be, nu: (be[i], 0, 0))],
            out_specs=pl.BlockSpec((MOE_BLOCK, d), lambda i, be, nu: (i, 0)),
            scratch_shapes=[pltpu.VMEM((d, EXPERT_HIDDEN), BF16), pltpu.VMEM((d, EXPERT_HIDDEN), BF16),
                            pltpu.VMEM((EXPERT_HIDDEN, d), BF16)]),
        out_shape=jax.ShapeDtypeStruct((rows, d), F32),
        compiler_params=_params(1),
        name="moe",
    )(block_expert, n_used, xs, w_gate[0], w_up[0], w_down[0])

    def combine(d0, d1, x1, gate, gate_spec, rf):
        n_tok = x1.shape[0]
        return pl.pallas_call(
            _combine_kernel,
            grid_spec=pltpu.PrefetchScalarGridSpec(
                num_scalar_prefetch=2,
                grid=(n_tok // T,),
                in_specs=[pl.BlockSpec((T, d), lambda i, *_: (i, 0)), gate_spec,
                          pl.BlockSpec((T, TOP_K), lambda i, *_: (i, 0)),
                          pl.BlockSpec((1, d), lambda i, *_: (0, 0)),
                          pl.BlockSpec(memory_space=pl.ANY)],
                out_specs=pl.BlockSpec((T, d), lambda i, *_: (i, 0)),
                scratch_shapes=[pltpu.VMEM((2, TOP_K, T, d), F32), pltpu.SemaphoreType.DMA((2,))]),
            out_shape=jax.ShapeDtypeStruct((n_tok, d), F32),
            compiler_params=_params(1),
            name="combine",
        )(d0, d1, x1, gate, rf[0:TOP_K].T, gf, yb)

    gate_p = mod_p[:, 5]
    y_p = combine(d0_p, d1_p, x1_p, gate_p, pl.BlockSpec((None, 1, d), lambda i, *_: (i // steps, 0, 0)), rf_p)
    y_s = combine(d0_s, d1_s, x1_s, mod_s[5], pl.BlockSpec((T, d), lambda i, *_: (i, 0)), rf_s)

    kv_shape = (1, nb, WINDOW, B_KV_HEADS, HEAD_DIM)
    new_shape = (1, ns, dec, B_KV_HEADS, HEAD_DIM)
    return (y_p.reshape(nb, seq, d), y_s.reshape(ns, dec, d),
            k_last.reshape(kv_shape), v_last.reshape(kv_shape),
            k_s.reshape(new_shape), v_s.reshape(new_shape),
            vn_s.reshape(1, ns, dec, A_WIDTH))
```
